```python
import jax, jax.numpy as jnp
from jax import lax
import numpy as np

D_MODEL = 1024
BATCH = 8
SEQ = 4096
DEPTH = 4

CHUNK = 64
N_BRANCH = 3
A_WIDTH = 512
A_GROUPS = 4
A_BLOCK = 128
B_WIDTH = 512
B_CONV = 3
C_WIDTH = 512
C_CONV = 31
D_FF = 2816
N_ADA = 9
EPS = 1e-6
IN_COLS = 2 * A_WIDTH + 3 * B_WIDTH + 2 * C_WIDTH + N_BRANCH * D_MODEL

kernel_name = "hybrid_gmlp_shortconv_conformer_macaron_block"


def rmsnorm(x, g):
    xf = x.astype(jnp.float32)
    y = xf * lax.rsqrt(jnp.mean(xf * xf, axis=-1, keepdims=True) + EPS)
    return (y * g.astype(jnp.float32)).astype(x.dtype)


def layernorm(x, g, b):
    xf = x.astype(jnp.float32)
    mu = jnp.mean(xf, axis=-1, keepdims=True)
    var = jnp.mean(jnp.square(xf - mu), axis=-1, keepdims=True)
    y = (xf - mu) * lax.rsqrt(var + EPS)
    return (y * g.astype(jnp.float32) + b.astype(jnp.float32)).astype(x.dtype)


def ada_norm(x, g, shift, scale):
    return rmsnorm(x, g) * (1.0 + scale[:, None, :]) + shift[:, None, :]


def swiglu(h, w13, w2):
    a, b = jnp.split(h @ w13, 2, axis=-1)
    return (jax.nn.silu(a) * b) @ w2


def causal_depthwise_conv(z, w):
    k_width, ch = w.shape
    return lax.conv_general_dilated(
        z, w[:, None, :].astype(z.dtype), window_strides=(1,),
        padding=[(k_width - 1, 0)], dimension_numbers=("NWC", "WIO", "NWC"),
        feature_group_count=ch)


def spatial_gating(u, v, ln_g, ln_b, ws, bs):
    bsz, seq, _ = u.shape
    v = layernorm(v, ln_g, ln_b)
    vb = v.reshape(bsz, seq // A_BLOCK, A_BLOCK, A_GROUPS, A_WIDTH // A_GROUPS)
    pos_chunk = jnp.arange(A_BLOCK) // CHUNK
    mask = pos_chunk[:, None] >= pos_chunk[None, :]
    w = jnp.where(mask[None], ws, jnp.zeros((), ws.dtype))
    sv = jnp.einsum("gpq,bnqgc->bnpgc", w, vb) + bs.T[None, None, :, :, None]
    return u * sv.reshape(bsz, seq, A_WIDTH)


def setup_inputs(seed: int = 0) -> dict:
    key = jax.random.key(seed)
    ks = jax.random.split(key, 32)

    def nrm(k, shape, s):
        return jax.random.normal(k, shape, jnp.float32) * s

    def gain(k, shape):
        return 1.0 + 0.01 * jax.random.normal(k, shape, jnp.float32)

    d, f, L = D_MODEL, D_FF, DEPTH
    return {
        "x": nrm(ks[0], (BATCH, SEQ, d), 1.0),
        "c": nrm(ks[1], (BATCH, d), 1.0),
        "w_ada": nrm(ks[2], (L, d, N_ADA * d), d ** -0.5),
        "b_ada": nrm(ks[3], (L, N_ADA * d), 0.01),
        "g_ffn1": gain(ks[4], (L, d)),
        "ffn1_w13": nrm(ks[5], (L, d, 2 * f), d ** -0.5),
        "ffn1_w2": nrm(ks[6], (L, f, d), f ** -0.5),
        "g_mix": gain(ks[7], (L, d)),
        "w_in": nrm(ks[8], (L, d, IN_COLS), d ** -0.5),
        "a_ln_g": gain(ks[9], (L, A_WIDTH)),
        "a_ln_b": nrm(ks[10], (L, A_WIDTH), 0.01),
        "a_ws": nrm(ks[11], (L, A_GROUPS, A_BLOCK, A_BLOCK), A_BLOCK ** -0.5),
        "a_bs": gain(ks[12], (L, A_GROUPS, A_BLOCK)),
        "b_conv": nrm(ks[13], (L, B_CONV, B_WIDTH), B_CONV ** -0.5),
        "c_conv": nrm(ks[14], (L, C_CONV, C_WIDTH), C_CONV ** -0.5),
        "c_conv_b": nrm(ks[15], (L, C_WIDTH), 0.01),
        "c_ln_g": gain(ks[16], (L, C_WIDTH)),
        "c_ln_b": nrm(ks[17], (L, C_WIDTH), 0.01),
        "w_branch_a": nrm(ks[18], (L, A_WIDTH, d), A_WIDTH ** -0.5),
        "w_branch_b": nrm(ks[19], (L, B_WIDTH, d), B_WIDTH ** -0.5),
        "w_branch_c": nrm(ks[20], (L, C_WIDTH, d), C_WIDTH ** -0.5),
        "w_o": nrm(ks[21], (L, d, d), d ** -0.5),
        "g_ffn2": gain(ks[22], (L, d)),
        "ffn2_w13": nrm(ks[23], (L, d, 2 * f), d ** -0.5),
        "ffn2_w2": nrm(ks[24], (L, f, d), f ** -0.5),
        "g_final": gain(ks[25], (d,)),
    }


def reference(x, c, w_ada, b_ada, g_ffn1, ffn1_w13, ffn1_w2, g_mix, w_in,
              a_ln_g, a_ln_b, a_ws, a_bs, b_conv, c_conv, c_conv_b, c_ln_g, c_ln_b,
              w_branch_a, w_branch_b, w_branch_c, w_o, g_ffn2, ffn2_w13, ffn2_w2,
              g_final):
    bsz, seq, d = x.shape
    splits = np.cumsum([A_WIDTH, A_WIDTH, B_WIDTH, B_WIDTH, B_WIDTH, C_WIDTH, C_WIDTH]).tolist()
    c_act = jax.nn.silu(c)
    for l in range(DEPTH):
        ada = (c_act @ w_ada[l] + b_ada[l]).reshape(bsz, N_ADA, d)

        h = ada_norm(x, g_ffn1[l], ada[:, 0], ada[:, 1])
        x = x + 0.5 * ada[:, 2][:, None, :] * swiglu(h, ffn1_w13[l], ffn1_w2[l])

        h = ada_norm(x, g_mix[l], ada[:, 3], ada[:, 4])
        proj = h @ w_in[l]
        a_u, a_v, b_b, b_c, b_h, c_a, c_g, gates = jnp.split(proj, splits, axis=-1)

        y_a = spatial_gating(jax.nn.gelu(a_u, approximate=False), jax.nn.gelu(a_v, approximate=False),
                             a_ln_g[l], a_ln_b[l], a_ws[l], a_bs[l])
        y_b = b_b * causal_depthwise_conv(b_c * b_h, b_conv[l])
        z = c_a * jax.nn.sigmoid(c_g)
        z = causal_depthwise_conv(z, c_conv[l]) + c_conv_b[l]
        y_c = jax.nn.silu(layernorm(z, c_ln_g[l], c_ln_b[l]))

        gates = jax.nn.sigmoid(gates).reshape(bsz, seq, N_BRANCH, d)
        merged = (gates[:, :, 0] * (y_a @ w_branch_a[l])
                  + gates[:, :, 1] * (y_b @ w_branch_b[l])
                  + gates[:, :, 2] * (y_c @ w_branch_c[l]))
        x = x + ada[:, 5][:, None, :] * (merged @ w_o[l])

        h = ada_norm(x, g_ffn2[l], ada[:, 6], ada[:, 7])
        x = x + 0.5 * ada[:, 8][:, None, :] * swiglu(h, ffn2_w13[l], ffn2_w2[l])

    return rmsnorm(x, g_final)
```

```python
import functools
import math

import jax
import jax.numpy as jnp
from jax import lax
from jax.experimental import pallas as pl
from jax.experimental.pallas import tpu as pltpu

D_MODEL = 1024
DEPTH = 4
CHUNK = 64
N_BRANCH = 3
A_WIDTH = 512
A_GROUPS = 4
A_BLOCK = 128
A_GROUP_WIDTH = A_WIDTH // A_GROUPS
B_WIDTH = 512
B_CONV = 3
C_WIDTH = 512
C_CONV = 31
D_FF = 2816
N_ADA = 9
EPS = 1e-6

COL_A = 0
COL_B = 2 * A_WIDTH
COL_C = COL_B + 3 * B_WIDTH
COL_G = COL_C + 2 * C_WIDTH
IN_COLS = COL_G + N_BRANCH * D_MODEL

V7X_SUBLANES = 8
V7X_MXU_WIDTH = 256
V7X_VMEM_LIMIT_BYTES = 56 * 1024 * 1024

TOKEN_TILE = 512
FF_CHUNK = V7X_MXU_WIDTH
CONV_ROWS = 64
B_HALO = V7X_SUBLANES
C_HALO = 32
ADA_COL_TILE = 1024

_F32 = jnp.float32
_BF16 = jnp.bfloat16


def _sigmoid(v):
    return 1.0 / (1.0 + jnp.exp(-v))


def _gelu_exact(v):
    return 0.5 * v * (1.0 + lax.erf(v * (1.0 / math.sqrt(2.0))))


def _rms_scale(v):
    return lax.rsqrt(jnp.mean(v * v, axis=-1, keepdims=True) + EPS)


def _layernorm(v, g, b):
    mu = jnp.mean(v, axis=-1, keepdims=True)
    d = v - mu
    var = jnp.mean(d * d, axis=-1, keepdims=True)
    return d * lax.rsqrt(var + EPS) * g + b


def _dot(a, b):
    return jnp.dot(a, b, preferred_element_type=_F32)


def _ada_kernel(c_ref, w_ref, b_ref, o_ref):
    c = c_ref[...]
    c_act = c * _sigmoid(c)
    o_ref[0] = jnp.dot(c_act, w_ref[0], preferred_element_type=_F32,
                       precision=lax.Precision.HIGHEST) + b_ref[0]


def _ada_call(c, w_ada, b_ada):
    depth, d, n = w_ada.shape
    bsz = c.shape[0]
    return pl.pallas_call(
        _ada_kernel,
        grid=(depth, n // ADA_COL_TILE),
        in_specs=[
            pl.BlockSpec((bsz, d), lambda l, j: (0, 0)),
            pl.BlockSpec((1, d, ADA_COL_TILE), lambda l, j: (l, 0, j)),
            pl.BlockSpec((1, 1, ADA_COL_TILE), lambda l, j: (l, 0, j)),
        ],
        out_specs=pl.BlockSpec((1, bsz, ADA_COL_TILE), lambda l, j: (l, 0, j)),
        out_shape=jax.ShapeDtypeStruct((depth, bsz, n), _F32),
        name="ada",
    )(c, w_ada, b_ada.reshape(depth, 1, n))


def _ffn_kernel(x_ref, ada_ref, g_ref, w13_ref, w2_ref, gf_ref, o_ref,
                h_ref, hid_ref, *, ada_base, final_norm):
    x = x_ref[0]
    shift = ada_ref[0, 0, ada_base:ada_base + 1, :]
    scale = ada_ref[0, 0, ada_base + 1:ada_base + 2, :]
    gate = ada_ref[0, 0, ada_base + 2:ada_base + 3, :]
    h = (x * _rms_scale(x)) * g_ref[0] * (1.0 + scale) + shift
    h_ref[...] = h.astype(_BF16)
    for j in range(D_FF // FF_CHUNK):
        lo = j * FF_CHUNK
        a = _dot(h_ref[...], w13_ref[0, :, lo:lo + FF_CHUNK])
        b = _dot(h_ref[...], w13_ref[0, :, D_FF + lo:D_FF + lo + FF_CHUNK])
        hid_ref[:, lo:lo + FF_CHUNK] = (a * _sigmoid(a) * b).astype(_BF16)
    y = _dot(hid_ref[...], w2_ref[0])
    out = x + (0.5 * gate) * y
    if final_norm:
        out = (out * _rms_scale(out)) * gf_ref[...]
    o_ref[0] = out


def _ffn_call(x, ada, g, w13, w2, g_final, *, layer, ada_base, final_norm):
    bsz, seq, d = x.shape
    tm = TOKEN_TILE
    const = dict(pipeline_mode=pl.Buffered(1))
    return pl.pallas_call(
        functools.partial(_ffn_kernel, ada_base=ada_base, final_norm=final_norm),
        grid=(bsz, seq // tm),
        in_specs=[
            pl.BlockSpec((1, tm, d), lambda b, s: (b, s, 0)),
            pl.BlockSpec((1, 1, N_ADA, d), lambda b, s: (layer, b, 0, 0)),
            pl.BlockSpec((1, 1, d), lambda b, s: (layer, 0, 0), **const),
            pl.BlockSpec((1, d, 2 * D_FF), lambda b, s: (layer, 0, 0), **const),
            pl.BlockSpec((1, D_FF, d), lambda b, s: (layer, 0, 0), **const),
            pl.BlockSpec((1, d), lambda b, s: (0, 0), **const),
        ],
        out_specs=pl.BlockSpec((1, tm, d), lambda b, s: (b, s, 0)),
        out_shape=jax.ShapeDtypeStruct(x.shape, x.dtype),
        scratch_shapes=[
            pltpu.VMEM((tm, d), _BF16),
            pltpu.VMEM((tm, D_FF), _BF16),
        ],
        compiler_params=pltpu.CompilerParams(
            dimension_semantics=("arbitrary", "arbitrary"),
            vmem_limit_bytes=V7X_VMEM_LIMIT_BYTES),
        name="ffn",
    )(x, ada, g, w13, w2, g_final)


def _mix_kernel(x_ref, ada_ref, g_ref, win_ref, alng_ref, alnb_ref, ws_ref, bst_ref,
                bconv_ref, cconv_ref, cconvb_ref, clng_ref, clnb_ref,
                wa_ref, wb_ref, wc_ref, wo_ref, o_ref,
                h_ref, ss_ref, zs_ref, zc_ref):
    tm = x_ref.shape[1]

    @pl.when(pl.program_id(1) == 0)
    def _():
        ss_ref[0:B_HALO, :] = jnp.zeros((B_HALO, B_WIDTH), _F32)
        zs_ref[0:C_HALO, :] = jnp.zeros((C_HALO, C_WIDTH), _F32)

    x = x_ref[0]
    shift = ada_ref[0, 0, 3:4, :]
    scale = ada_ref[0, 0, 4:5, :]
    gate = ada_ref[0, 0, 5:6, :]
    h = (x * _rms_scale(x)) * g_ref[0] * (1.0 + scale) + shift
    h_ref[...] = h.astype(_BF16)

    uv = _dot(h_ref[...], win_ref[0, :, COL_A:COL_A + 2 * A_WIDTH])
    u = _gelu_exact(uv[:, :A_WIDTH])
    v = _layernorm(_gelu_exact(uv[:, A_WIDTH:]), alng_ref[0], alnb_ref[0]).astype(_BF16)
    p_chunk = lax.broadcasted_iota(jnp.int32, (A_BLOCK, A_BLOCK), 0) // CHUNK
    q_chunk = lax.broadcasted_iota(jnp.int32, (A_BLOCK, A_BLOCK), 1) // CHUNK
    causal = p_chunk >= q_chunk
    sv_groups = []
    for g in range(A_GROUPS):
        w_g = jnp.where(causal, ws_ref[0, g], 0.0).astype(_BF16)
        bias_g = bst_ref[0, :, g:g + 1]
        cols = slice(g * A_GROUP_WIDTH, (g + 1) * A_GROUP_WIDTH)
        blocks = [
            _dot(w_g, v[n * A_BLOCK:(n + 1) * A_BLOCK, cols]) + bias_g
            for n in range(tm // A_BLOCK)
        ]
        sv_groups.append(jnp.concatenate(blocks, axis=0))
    y_a = (u * jnp.concatenate(sv_groups, axis=1)).astype(_BF16)
    gate_a = _sigmoid(_dot(h_ref[...], win_ref[0, :, COL_G:COL_G + D_MODEL]))
    merged = gate_a * _dot(y_a, wa_ref[0])

    bp = _dot(h_ref[...], win_ref[0, :, COL_B:COL_B + 3 * B_WIDTH])
    ss_ref[B_HALO:B_HALO + tm, :] = bp[:, B_WIDTH:2 * B_WIDTH] * bp[:, 2 * B_WIDTH:]
    conv_b = None
    for k in range(B_CONV):
        off = B_HALO - (B_CONV - 1) + k
        term = bconv_ref[0, k:k + 1, :] * ss_ref[off:off + tm, :]
        conv_b = term if conv_b is None else conv_b + term
    ss_ref[0:B_HALO, :] = ss_ref[tm:tm + B_HALO, :]
    y_b = (bp[:, :B_WIDTH] * conv_b).astype(_BF16)
    gate_b = _sigmoid(_dot(h_ref[...], win_ref[0, :, COL_G + D_MODEL:COL_G + 2 * D_MODEL]))
    merged = merged + gate_b * _dot(y_b, wb_ref[0])

    cp = _dot(h_ref[...], win_ref[0, :, COL_C:COL_C + 2 * C_WIDTH])
    zs_ref[C_HALO:C_HALO + tm, :] = cp[:, :C_WIDTH] * _sigmoid(cp[:, C_WIDTH:])
    for r in range(tm // CONV_ROWS):
        acc = None
        for k in range(C_CONV):
            off = r * CONV_ROWS + C_HALO - (C_CONV - 1) + k
            term = cconv_ref[0, k:k + 1, :] * zs_ref[off:off + CONV_ROWS, :]
            acc = term if acc is None else acc + term
        zc_ref[r * CONV_ROWS:(r + 1) * CONV_ROWS, :] = acc + cconvb_ref[0]
    zs_ref[0:C_HALO, :] = zs_ref[tm:tm + C_HALO, :]
    zn = _layernorm(zc_ref[...], clng_ref[0], clnb_ref[0])
    y_c = (zn * _sigmoid(zn)).astype(_BF16)
    gate_c = _sigmoid(_dot(h_ref[...], win_ref[0, :, COL_G + 2 * D_MODEL:COL_G + 3 * D_MODEL]))
    merged = merged + gate_c * _dot(y_c, wc_ref[0])

    o_ref[0] = x + gate * _dot(merged.astype(_BF16), wo_ref[0])


def _mix_call(x, ada, g, w_in, a_ln_g, a_ln_b, a_ws, a_bs_t, b_conv, c_conv, c_conv_b,
              c_ln_g, c_ln_b, w_a, w_b, w_c, w_o, *, layer):
    bsz, seq, d = x.shape
    tm = TOKEN_TILE
    const = dict(pipeline_mode=pl.Buffered(1))

    def layer_block(arr):
        shape = (1,) + arr.shape[1:]
        zeros = (0,) * (arr.ndim - 1)
        return pl.BlockSpec(shape, lambda b, s: (layer,) + zeros, **const)

    params = (g, w_in, a_ln_g, a_ln_b, a_ws, a_bs_t, b_conv, c_conv, c_conv_b,
              c_ln_g, c_ln_b, w_a, w_b, w_c, w_o)
    return pl.pallas_call(
        _mix_kernel,
        grid=(bsz, seq // tm),
        in_specs=[
            pl.BlockSpec((1, tm, d), lambda b, s: (b, s, 0)),
            pl.BlockSpec((1, 1, N_ADA, d), lambda b, s: (layer, b, 0, 0)),
        ] + [layer_block(p) for p in params],
        out_specs=pl.BlockSpec((1, tm, d), lambda b, s: (b, s, 0)),
        out_shape=jax.ShapeDtypeStruct(x.shape, x.dtype),
        scratch_shapes=[
            pltpu.VMEM((tm, d), _BF16),
            pltpu.VMEM((B_HALO + tm, B_WIDTH), _F32),
            pltpu.VMEM((C_HALO + tm, C_WIDTH), _F32),
            pltpu.VMEM((tm, C_WIDTH), _F32),
        ],
        compiler_params=pltpu.CompilerParams(
            dimension_semantics=("arbitrary", "arbitrary"),
            vmem_limit_bytes=V7X_VMEM_LIMIT_BYTES),
        name="mix",
    )(x, ada, *params)


def kernel(x, c, w_ada, b_ada, g_ffn1, ffn1_w13, ffn1_w2, g_mix, w_in, a_ln_g, a_ln_b, a_ws, a_bs, b_conv, c_conv, c_conv_b, c_ln_g, c_ln_b, w_branch_a, w_branch_b, w_branch_c, w_o, g_ffn2, ffn2_w13, ffn2_w2, g_final):
    bsz, seq, d = x.shape
    depth = w_ada.shape[0]
    assert (d, depth) == (D_MODEL, DEPTH) and seq % TOKEN_TILE == 0
    assert TOKEN_TILE % A_BLOCK == 0 and TOKEN_TILE % CONV_ROWS == 0

    ada = _ada_call(c, w_ada, b_ada).reshape(depth, bsz, N_ADA, d)

    def row(p):
        return p.reshape(depth, 1, p.shape[-1])

    bf = lambda w: w.astype(_BF16)
    ffn1_w13, ffn1_w2, ffn2_w13, ffn2_w2 = map(bf, (ffn1_w13, ffn1_w2, ffn2_w13, ffn2_w2))
    w_in, w_branch_a, w_branch_b, w_branch_c, w_o = map(
        bf, (w_in, w_branch_a, w_branch_b, w_branch_c, w_o))
    a_bs_t = jnp.swapaxes(a_bs, 1, 2)
    g_final = g_final.reshape(1, d)

    for l in range(depth):
        x = _ffn_call(x, ada, row(g_ffn1), ffn1_w13, ffn1_w2, g_final,
                      layer=l, ada_base=0, final_norm=False)
        x = _mix_call(x, ada, row(g_mix), w_in, row(a_ln_g), row(a_ln_b), a_ws, a_bs_t,
                      b_conv, c_conv, row(c_conv_b), row(c_ln_g), row(c_ln_b),
                      w_branch_a, w_branch_b, w_branch_c, w_o, layer=l)
        x = _ffn_call(x, ada, row(g_ffn2), ffn2_w13, ffn2_w2, g_final,
                      layer=l, ada_base=6, final_norm=(l == depth - 1))
    return x
```

```python
import functools
import math

import jax
import jax.numpy as jnp
from jax import lax
from jax.experimental import pallas as pl
from jax.experimental.pallas import tpu as pltpu

D_MODEL = 1024
DEPTH = 4
CHUNK = 64
N_BRANCH = 3
A_WIDTH = 512
A_GROUPS = 4
A_BLOCK = 128
A_GROUP_WIDTH = A_WIDTH // A_GROUPS
B_WIDTH = 512
B_CONV = 3
C_WIDTH = 512
C_CONV = 31
D_FF = 2816
N_ADA = 9
EPS = 1e-6

COL_A = 0
COL_B = 2 * A_WIDTH
COL_C = COL_B + 3 * B_WIDTH
COL_G = COL_C + 2 * C_WIDTH
IN_COLS = COL_G + N_BRANCH * D_MODEL

V7X_SUBLANES = 8
V7X_LANES = 128
V7X_MXU_WIDTH = 256
V7X_VMEM_LIMIT_BYTES = 56 * 1024 * 1024

TOKEN_TILE = 512
FF_CHUNK = V7X_MXU_WIDTH
CONV_ROWS = 64
B_HALO = V7X_SUBLANES
C_HALO = 32
ADA_COL_TILE = 1024

_F32 = jnp.float32
_BF16 = jnp.bfloat16


def _sigmoid(v):
    return 1.0 / (1.0 + jnp.exp(-v))


def _gelu_exact(v):
    return 0.5 * v * (1.0 + lax.erf(v * (1.0 / math.sqrt(2.0))))


def _rms_scale(v):
    return lax.rsqrt(jnp.mean(v * v, axis=-1, keepdims=True) + EPS)


def _layernorm(v, g, b):
    mu = jnp.mean(v, axis=-1, keepdims=True)
    d = v - mu
    var = jnp.mean(d * d, axis=-1, keepdims=True)
    return d * lax.rsqrt(var + EPS) * g + b


def _dot(a, b):
    return jnp.dot(a, b, preferred_element_type=_F32)


def _ada_kernel(c_ref, w_ref, b_ref, o_ref):
    c = c_ref[...]
    c_act = c * _sigmoid(c)
    o_ref[0] = jnp.dot(c_act, w_ref[0], preferred_element_type=_F32,
                       precision=lax.Precision.HIGHEST) + b_ref[0]


def _ada_call(c, w_ada, b_ada):
    depth, d, n = w_ada.shape
    bsz = c.shape[0]
    return pl.pallas_call(
        _ada_kernel,
        grid=(depth, n // ADA_COL_TILE),
        in_specs=[
            pl.BlockSpec((bsz, d), lambda l, j: (0, 0)),
            pl.BlockSpec((1, d, ADA_COL_TILE), lambda l, j: (l, 0, j)),
            pl.BlockSpec((1, 1, ADA_COL_TILE), lambda l, j: (l, 0, j)),
        ],
        out_specs=pl.BlockSpec((1, bsz, ADA_COL_TILE), lambda l, j: (l, 0, j)),
        out_shape=jax.ShapeDtypeStruct((depth, bsz, n), _F32),
        name="ada",
    )(c, w_ada, b_ada.reshape(depth, 1, n))


def _ffn_kernel(x_ref, ada_ref, g_ref, w13_ref, w2_ref, gf_ref, o_ref,
                h_ref, hid_ref, *, ada_base, final_norm):
    x = x_ref[0]
    shift = ada_ref[0, 0, ada_base:ada_base + 1, :]
    scale = ada_ref[0, 0, ada_base + 1:ada_base + 2, :]
    gate = ada_ref[0, 0, ada_base + 2:ada_base + 3, :]
    h = (x * _rms_scale(x)) * g_ref[0] * (1.0 + scale) + shift
    h_ref[...] = h.astype(_BF16)
    for j in range(D_FF // FF_CHUNK):
        lo = j * FF_CHUNK
        a = _dot(h_ref[...], w13_ref[0, :, lo:lo + FF_CHUNK])
        b = _dot(h_ref[...], w13_ref[0, :, D_FF + lo:D_FF + lo + FF_CHUNK])
        hid_ref[:, lo:lo + FF_CHUNK] = (a * _sigmoid(a) * b).astype(_BF16)
    y = _dot(hid_ref[...], w2_ref[0])
    out = x + (0.5 * gate) * y
    if final_norm:
        out = (out * _rms_scale(out)) * gf_ref[...]
    o_ref[0] = out


def _ffn_call(x, ada, g, w13, w2, g_final, *, layer, ada_base, final_norm):
    bsz, seq, d = x.shape
    tm = TOKEN_TILE
    const = dict(pipeline_mode=pl.Buffered(1))
    return pl.pallas_call(
        functools.partial(_ffn_kernel, ada_base=ada_base, final_norm=final_norm),
        grid=(bsz, seq // tm),
        in_specs=[
            pl.BlockSpec((1, tm, d), lambda b, s: (b, s, 0)),
            pl.BlockSpec((1, 1, N_ADA, d), lambda b, s: (layer, b, 0, 0)),
            pl.BlockSpec((1, 1, d), lambda b, s: (layer, 0, 0), **const),
            pl.BlockSpec((1, d, 2 * D_FF), lambda b, s: (layer, 0, 0), **const),
            pl.BlockSpec((1, D_FF, d), lambda b, s: (layer, 0, 0), **const),
            pl.BlockSpec((1, d), lambda b, s: (0, 0), **const),
        ],
        out_specs=pl.BlockSpec((1, tm, d), lambda b, s: (b, s, 0)),
        out_shape=jax.ShapeDtypeStruct(x.shape, x.dtype),
        scratch_shapes=[
            pltpu.VMEM((tm, d), _BF16),
            pltpu.VMEM((tm, D_FF), _BF16),
        ],
        compiler_params=pltpu.CompilerParams(
            dimension_semantics=("arbitrary", "arbitrary"),
            vmem_limit_bytes=V7X_VMEM_LIMIT_BYTES),
        name="ffn",
    )(x, ada, g, w13, w2, g_final)


def _conv31(zs_ref, w_ref, bias_ref, out_ref, tm):
    lead = C_HALO - (C_CONV - 1)
    for r in range(tm // CONV_ROWS):
        t0 = r * CONV_ROWS
        for c in range(C_WIDTH // V7X_LANES):
            lanes = slice(c * V7X_LANES, (c + 1) * V7X_LANES)
            zin = zs_ref[t0:t0 + CONV_ROWS + C_HALO, lanes]
            y = jnp.broadcast_to(bias_ref[0, :, lanes], (CONV_ROWS, V7X_LANES))
            for b in range(V7X_SUBLANES):
                rows = CONV_ROWS + (V7X_SUBLANES if b else 0)
                q = None
                for a in range(C_HALO // V7X_SUBLANES + 1):
                    k = V7X_SUBLANES * a + b - lead
                    if 0 <= k < C_CONV:
                        lo = V7X_SUBLANES * a
                        term = w_ref[0, k:k + 1, lanes] * zin[lo:lo + rows]
                        q = term if q is None else q + term
                y = y + q[b:b + CONV_ROWS]
            out_ref[t0:t0 + CONV_ROWS, lanes] = y
        yield


def _mix_kernel(x_ref, ada_ref, g_ref, win_ref, alng_ref, alnb_ref, ws_ref, bst_ref,
                bconv_ref, cconv_ref, cconvb_ref, clng_ref, clnb_ref,
                wa_ref, wb_ref, wc_ref, wo_ref, o_ref,
                h_ref, ss_ref, zs_ref, zc_ref):
    tm = x_ref.shape[1]

    @pl.when(pl.program_id(1) == 0)
    def _():
        ss_ref[0:B_HALO, :] = jnp.zeros((B_HALO, B_WIDTH), _F32)
        zs_ref[0:C_HALO, :] = jnp.zeros((C_HALO, C_WIDTH), _F32)

    x = x_ref[0]
    shift = ada_ref[0, 0, 3:4, :]
    scale = ada_ref[0, 0, 4:5, :]
    gate = ada_ref[0, 0, 5:6, :]
    h = (x * _rms_scale(x)) * g_ref[0] * (1.0 + scale) + shift
    h_ref[...] = h.astype(_BF16)

    cp = _dot(h_ref[...], win_ref[0, :, COL_C:COL_C + 2 * C_WIDTH])
    zs_ref[C_HALO:C_HALO + tm, :] = cp[:, :C_WIDTH] * _sigmoid(cp[:, C_WIDTH:])
    conv = _conv31(zs_ref, cconv_ref, cconvb_ref, zc_ref, tm)

    def overlap_conv(chunks):
        for _ in range(chunks):
            next(conv, None)

    overlap_conv(2)
    uv = _dot(h_ref[...], win_ref[0, :, COL_A:COL_A + 2 * A_WIDTH])
    overlap_conv(2)
    gate_a = _sigmoid(_dot(h_ref[...], win_ref[0, :, COL_G:COL_G + D_MODEL]))
    overlap_conv(2)
    bp = _dot(h_ref[...], win_ref[0, :, COL_B:COL_B + 3 * B_WIDTH])
    overlap_conv(tm // CONV_ROWS)
    zs_ref[0:C_HALO, :] = zs_ref[tm:tm + C_HALO, :]
    gate_b = _sigmoid(_dot(h_ref[...], win_ref[0, :, COL_G + D_MODEL:COL_G + 2 * D_MODEL]))
    gate_c = _sigmoid(_dot(h_ref[...], win_ref[0, :, COL_G + 2 * D_MODEL:COL_G + 3 * D_MODEL]))

    u = _gelu_exact(uv[:, :A_WIDTH])
    v = _layernorm(_gelu_exact(uv[:, A_WIDTH:]), alng_ref[0], alnb_ref[0]).astype(_BF16)
    p_chunk = lax.broadcasted_iota(jnp.int32, (A_BLOCK, A_BLOCK), 0) // CHUNK
    q_chunk = lax.broadcasted_iota(jnp.int32, (A_BLOCK, A_BLOCK), 1) // CHUNK
    causal = p_chunk >= q_chunk
    sv_groups = []
    for g in range(A_GROUPS):
        w_g = jnp.where(causal, ws_ref[0, g], 0.0).astype(_BF16)
        bias_g = bst_ref[0, :, g:g + 1]
        cols = slice(g * A_GROUP_WIDTH, (g + 1) * A_GROUP_WIDTH)
        blocks = [
            _dot(w_g, v[n * A_BLOCK:(n + 1) * A_BLOCK, cols]) + bias_g
            for n in range(tm // A_BLOCK)
        ]
        sv_groups.append(jnp.concatenate(blocks, axis=0))
    y_a = (u * jnp.concatenate(sv_groups, axis=1)).astype(_BF16)
    merged = gate_a * _dot(y_a, wa_ref[0])

    ss_ref[B_HALO:B_HALO + tm, :] = bp[:, B_WIDTH:2 * B_WIDTH] * bp[:, 2 * B_WIDTH:]
    conv_b = None
    for k in range(B_CONV):
        off = B_HALO - (B_CONV - 1) + k
        term = bconv_ref[0, k:k + 1, :] * ss_ref[off:off + tm, :]
        conv_b = term if conv_b is None else conv_b + term
    ss_ref[0:B_HALO, :] = ss_ref[tm:tm + B_HALO, :]
    y_b = (bp[:, :B_WIDTH] * conv_b).astype(_BF16)
    merged = merged + gate_b * _dot(y_b, wb_ref[0])

    zn = _layernorm(zc_ref[...], clng_ref[0], clnb_ref[0])
    y_c = (zn * _sigmoid(zn)).astype(_BF16)
    merged = merged + gate_c * _dot(y_c, wc_ref[0])

    o_ref[0] = x + gate * _dot(merged.astype(_BF16), wo_ref[0])


def _mix_call(x, ada, g, w_in, a_ln_g, a_ln_b, a_ws, a_bs_t, b_conv, c_conv, c_conv_b,
              c_ln_g, c_ln_b, w_a, w_b, w_c, w_o, *, layer):
    bsz, seq, d = x.shape
    tm = TOKEN_TILE
    const = dict(pipeline_mode=pl.Buffered(1))

    def layer_block(arr):
        shape = (1,) + arr.shape[1:]
        zeros = (0,) * (arr.ndim - 1)
        return pl.BlockSpec(shape, lambda b, s: (layer,) + zeros, **const)

    params = (g, w_in, a_ln_g, a_ln_b, a_ws, a_bs_t, b_conv, c_conv, c_conv_b,
              c_ln_g, c_ln_b, w_a, w_b, w_c, w_o)
    return pl.pallas_call(
        _mix_kernel,
        grid=(bsz, seq // tm),
        in_specs=[
            pl.BlockSpec((1, tm, d), lambda b, s: (b, s, 0)),
            pl.BlockSpec((1, 1, N_ADA, d), lambda b, s: (layer, b, 0, 0)),
        ] + [layer_block(p) for p in params],
        out_specs=pl.BlockSpec((1, tm, d), lambda b, s: (b, s, 0)),
        out_shape=jax.ShapeDtypeStruct(x.shape, x.dtype),
        scratch_shapes=[
            pltpu.VMEM((tm, d), _BF16),
            pltpu.VMEM((B_HALO + tm, B_WIDTH), _F32),
            pltpu.VMEM((C_HALO + tm, C_WIDTH), _F32),
            pltpu.VMEM((tm, C_WIDTH), _F32),
        ],
        compiler_params=pltpu.CompilerParams(
            dimension_semantics=("arbitrary", "arbitrary"),
            vmem_limit_bytes=V7X_VMEM_LIMIT_BYTES),
        name="mix",
    )(x, ada, *params)


def kernel(x, c, w_ada, b_ada, g_ffn1, ffn1_w13, ffn1_w2, g_mix, w_in, a_ln_g, a_ln_b, a_ws, a_bs, b_conv, c_conv, c_conv_b, c_ln_g, c_ln_b, w_branch_a, w_branch_b, w_branch_c, w_o, g_ffn2, ffn2_w13, ffn2_w2, g_final):
    bsz, seq, d = x.shape
    depth = w_ada.shape[0]
    assert (d, depth) == (D_MODEL, DEPTH) and seq % TOKEN_TILE == 0
    assert TOKEN_TILE % A_BLOCK == 0 and TOKEN_TILE % CONV_ROWS == 0

    ada = _ada_call(c, w_ada, b_ada).reshape(depth, bsz, N_ADA, d)

    def row(p):
        return p.reshape(depth, 1, p.shape[-1])

    bf = lambda w: w.astype(_BF16)
    ffn1_w13, ffn1_w2, ffn2_w13, ffn2_w2 = map(bf, (ffn1_w13, ffn1_w2, ffn2_w13, ffn2_w2))
    w_in, w_branch_a, w_branch_b, w_branch_c, w_o = map(
        bf, (w_in, w_branch_a, w_branch_b, w_branch_c, w_o))
    a_bs_t = jnp.swapaxes(a_bs, 1, 2)
    g_final = g_final.reshape(1, d)

    for l in range(depth):
        x = _ffn_call(x, ada, row(g_ffn1), ffn1_w13, ffn1_w2, g_final,
                      layer=l, ada_base=0, final_norm=False)
        x = _mix_call(x, ada, row(g_mix), w_in, row(a_ln_g), row(a_ln_b), a_ws, a_bs_t,
                      b_conv, c_conv, row(c_conv_b), row(c_ln_g), row(c_ln_b),
                      w_branch_a, w_branch_b, w_branch_c, w_o, layer=l)
        x = _ffn_call(x, ada, row(g_ffn2), ffn2_w13, ffn2_w2, g_final,
                      layer=l, ada_base=6, final_norm=(l == depth - 1))
    return x
```
